```python
import math
import jax, jax.numpy as jnp
from jax import lax
import numpy as np

D_MODEL = 1024
BATCH = 8
SEQ = 2048
DEPTH = 2

EPS = 1e-6
SSM_WIDTH = D_MODEL // 2
SSM_GROUP_SIZE = 16
SSM_GROUPS = SSM_WIDTH // SSM_GROUP_SIZE
SSM_STATE = 64
DT_MIN = 1e-3
DT_MAX = 1e-1
POOL_WIDTH = D_MODEL // 2
POOL_WINDOWS = (2, 4, 8, 16)
POOL_GROUPS = len(POOL_WINDOWS)
POOL_GROUP = POOL_WIDTH // POOL_GROUPS
N_IN = 2 * SSM_WIDTH + 2 * POOL_WIDTH + 2 * D_MODEL
SPLITS = (SSM_WIDTH, 2 * SSM_WIDTH, 2 * SSM_WIDTH + POOL_WIDTH,
          2 * SSM_WIDTH + 2 * POOL_WIDTH, 2 * SSM_WIDTH + 2 * POOL_WIDTH + D_MODEL)

kernel_name = "hawk_merge_s5_pool_hybrid"


def _rmsnorm(x, g):
    x32 = x.astype(jnp.float32)
    y = x32 * lax.rsqrt(jnp.mean(x32 * x32, axis=-1, keepdims=True) + EPS)
    return y.astype(x.dtype) * g


def _complex_linear_combine(left, right):
    a1r, a1i, b1r, b1i = left
    a2r, a2i, b2r, b2i = right
    ar = a2r * a1r - a2i * a1i
    ai = a2r * a1i + a2i * a1r
    br = a2r * b1r - a2i * b1i + b2r
    bi = a2r * b1i + a2i * b1r + b2i
    return ar, ai, br, bi


def _s5_branch(u, log_dt, lam_re, lam_im, b_re, b_im, c_re, c_im, d_skip, w_glu, b_glu):
    bsz, seq, _ = u.shape
    ug = u.reshape(bsz, seq, SSM_GROUPS, SSM_GROUP_SIZE)
    dt = jnp.exp(log_dt)[:, None]
    mag = jnp.exp(lam_re * dt)
    ang = lam_im * dt
    abar_re = mag * jnp.cos(ang)
    abar_im = mag * jnp.sin(ang)
    num_re = abar_re - 1.0
    num_im = abar_im
    den = lam_re * lam_re + lam_im * lam_im
    coef_re = (num_re * lam_re + num_im * lam_im) / den
    coef_im = (num_im * lam_re - num_re * lam_im) / den
    bbar_re = coef_re[..., None] * b_re - coef_im[..., None] * b_im
    bbar_im = coef_re[..., None] * b_im + coef_im[..., None] * b_re
    bu_re = jnp.einsum('blgc,gpc->blgp', ug, bbar_re)
    bu_im = jnp.einsum('blgc,gpc->blgp', ug, bbar_im)
    a_re = jnp.broadcast_to(abar_re, bu_re.shape)
    a_im = jnp.broadcast_to(abar_im, bu_im.shape)
    _, _, s_re, s_im = lax.associative_scan(_complex_linear_combine,
                                            (a_re, a_im, bu_re, bu_im), axis=1)
    y = (jnp.einsum('blgp,gcp->blgc', s_re, c_re)
         - jnp.einsum('blgp,gcp->blgc', s_im, c_im))
    y = y.reshape(bsz, seq, SSM_WIDTH) + d_skip * u
    y = jax.nn.gelu(y)
    return y * jax.nn.sigmoid(y @ w_glu + b_glu)


def _pool_branch(u, w_group, scale):
    bsz, seq, _ = u.shape
    u32 = u.astype(jnp.float32)
    cs = jnp.cumsum(u32, axis=1)
    pos = jnp.arange(seq)
    outs = []
    for gi, win in enumerate(POOL_WINDOWS):
        csg = cs[:, :, gi * POOL_GROUP:(gi + 1) * POOL_GROUP]
        shifted = jnp.pad(csg, ((0, 0), (win, 0), (0, 0)))[:, :seq]
        count = jnp.minimum(pos + 1, win).astype(jnp.float32)[None, :, None]
        mean = (csg - shifted) / count
        outs.append(mean - u32[:, :, gi * POOL_GROUP:(gi + 1) * POOL_GROUP])
    pooled = jnp.stack(outs, axis=2).astype(u.dtype)
    mixed = jnp.einsum('blgc,gcd->blgd', pooled, w_group).reshape(bsz, seq, POOL_WIDTH)
    return mixed * scale


def setup_inputs(seed: int = 0) -> dict:
    key = jax.random.key(seed)
    ks = jax.random.split(key, 24)
    f32 = jnp.float32
    nrm = lambda k, shape, std: (jax.random.normal(k, shape, f32) * std)
    x = jax.random.normal(ks[0], (BATCH, SEQ, D_MODEL), f32)
    norm_g = 1.0 + nrm(ks[1], (DEPTH, D_MODEL), 0.05)
    w_in = nrm(ks[2], (DEPTH, D_MODEL, N_IN), D_MODEL ** -0.5)
    b_in = nrm(ks[3], (DEPTH, N_IN), 0.02)
    ssm_log_dt = jax.random.uniform(ks[4], (DEPTH, SSM_GROUPS), f32,
                                    math.log(DT_MIN), math.log(DT_MAX))
    n_idx = jnp.arange(SSM_STATE, dtype=f32)
    ssm_lam_re = -0.5 + nrm(ks[5], (DEPTH, SSM_GROUPS, SSM_STATE), 0.01)
    ssm_lam_im = math.pi * n_idx[None, None, :] + nrm(ks[6], (DEPTH, SSM_GROUPS, SSM_STATE), 0.01)
    b_std = (2.0 * SSM_GROUP_SIZE) ** -0.5
    ssm_b_re = nrm(ks[7], (DEPTH, SSM_GROUPS, SSM_STATE, SSM_GROUP_SIZE), b_std)
    ssm_b_im = nrm(ks[8], (DEPTH, SSM_GROUPS, SSM_STATE, SSM_GROUP_SIZE), b_std)
    c_std = SSM_STATE ** -0.5
    ssm_c_re = nrm(ks[9], (DEPTH, SSM_GROUPS, SSM_GROUP_SIZE, SSM_STATE), c_std)
    ssm_c_im = nrm(ks[10], (DEPTH, SSM_GROUPS, SSM_GROUP_SIZE, SSM_STATE), c_std)
    ssm_d = nrm(ks[11], (DEPTH, SSM_WIDTH), 1.0)
    ssm_w_glu = nrm(ks[12], (DEPTH, SSM_WIDTH, SSM_WIDTH), SSM_WIDTH ** -0.5)
    ssm_b_glu = nrm(ks[13], (DEPTH, SSM_WIDTH), 0.02)
    pool_w = nrm(ks[14], (DEPTH, POOL_GROUPS, POOL_GROUP, POOL_GROUP), POOL_GROUP ** -0.5)
    pool_scale = 1.0 + nrm(ks[15], (DEPTH, POOL_WIDTH), 0.1)
    w_branch_a = nrm(ks[16], (DEPTH, SSM_WIDTH, D_MODEL), SSM_WIDTH ** -0.5)
    w_branch_b = nrm(ks[17], (DEPTH, POOL_WIDTH, D_MODEL), POOL_WIDTH ** -0.5)
    w_out = nrm(ks[18], (DEPTH, D_MODEL, D_MODEL), D_MODEL ** -0.5)
    final_norm_g = 1.0 + nrm(ks[19], (D_MODEL,), 0.05)
    return {"x": x, "norm_g": norm_g, "w_in": w_in, "b_in": b_in,
            "ssm_log_dt": ssm_log_dt, "ssm_lam_re": ssm_lam_re, "ssm_lam_im": ssm_lam_im,
            "ssm_b_re": ssm_b_re, "ssm_b_im": ssm_b_im, "ssm_c_re": ssm_c_re, "ssm_c_im": ssm_c_im,
            "ssm_d": ssm_d, "ssm_w_glu": ssm_w_glu, "ssm_b_glu": ssm_b_glu,
            "pool_w": pool_w, "pool_scale": pool_scale,
            "w_branch_a": w_branch_a, "w_branch_b": w_branch_b, "w_out": w_out,
            "final_norm_g": final_norm_g}


def reference(x, norm_g, w_in, b_in, ssm_log_dt, ssm_lam_re, ssm_lam_im, ssm_b_re, ssm_b_im,
              ssm_c_re, ssm_c_im, ssm_d, ssm_w_glu, ssm_b_glu, pool_w, pool_scale,
              w_branch_a, w_branch_b, w_out, final_norm_g):
    for l in range(DEPTH):
        h = _rmsnorm(x, norm_g[l])
        proj = h @ w_in[l] + b_in[l]
        ua, za, ub, zb, ga, gb = jnp.split(proj, SPLITS, axis=-1)
        ya = _s5_branch(ua, ssm_log_dt[l], ssm_lam_re[l], ssm_lam_im[l], ssm_b_re[l], ssm_b_im[l],
                        ssm_c_re[l], ssm_c_im[l], ssm_d[l], ssm_w_glu[l], ssm_b_glu[l])
        ya = ya * jax.nn.silu(za)
        yb = _pool_branch(ub, pool_w[l], pool_scale[l]) * jax.nn.silu(zb)
        merged = (jax.nn.sigmoid(ga) * (ya @ w_branch_a[l])
                  + jax.nn.sigmoid(gb) * (yb @ w_branch_b[l]))
        x = x + merged @ w_out[l]
    return _rmsnorm(x, final_norm_g)
```

```python
import functools

import jax
import jax.numpy as jnp
from jax import lax
from jax.experimental import pallas as pl
from jax.experimental.pallas import tpu as pltpu

D_MODEL = 1024
BATCH = 8
SEQ = 2048
DEPTH = 2
EPS = 1e-6
SSM_WIDTH = D_MODEL // 2
SSM_GROUP_SIZE = 16
SSM_GROUPS = SSM_WIDTH // SSM_GROUP_SIZE
SSM_STATE = 64
POOL_WIDTH = D_MODEL // 2
POOL_WINDOWS = (2, 4, 8, 16)
POOL_GROUP = POOL_WIDTH // len(POOL_WINDOWS)
N_IN = 2 * SSM_WIDTH + 2 * POOL_WIDTH + 2 * D_MODEL

SUBLANES = 8
LANES = 128
assert BATCH == SUBLANES

GROUPS_PER_SLAB = LANES // SSM_GROUP_SIZE
N_SLABS = SSM_GROUPS // GROUPS_PER_SLAB
SLAB_STATE = GROUPS_PER_SLAB * SSM_STATE
STATE_LANES = 2 * SSM_GROUPS * SSM_STATE

TIME_TILE = 32
ROWS = TIME_TILE * BATCH
HALO_STEPS = 16
HALO_ROWS = HALO_STEPS * BATCH
VMEM_LIMIT_BYTES = 48 * 1024 * 1024


def _dot(a, b):
    return jnp.dot(a, b, preferred_element_type=jnp.float32)


def _discretise_kernel(log_dt_ref, lam_re_ref, lam_im_ref, b_re_ref, b_im_ref,
                       abar_re_ref, abar_im_ref, bbar_re_ref, bbar_im_ref):
    dt = jnp.exp(log_dt_ref[...])
    lam_re = lam_re_ref[...]
    lam_im = lam_im_ref[...]
    mag = jnp.exp(lam_re * dt)
    ang = lam_im * dt
    abar_re = mag * jnp.cos(ang)
    abar_im = mag * jnp.sin(ang)
    num_re = abar_re - 1.0
    num_im = abar_im
    den = lam_re * lam_re + lam_im * lam_im
    coef_re = (num_re * lam_re + num_im * lam_im) / den
    coef_im = (num_im * lam_re - num_re * lam_im) / den
    b_re = b_re_ref[...]
    b_im = b_im_ref[...]
    abar_re_ref[...] = abar_re
    abar_im_ref[...] = abar_im
    bbar_re_ref[...] = coef_re * b_re - coef_im * b_im
    bbar_im_ref[...] = coef_re * b_im + coef_im * b_re


def _discretise(log_dt, lam_re, lam_im, b_re, b_im):
    n = SSM_GROUPS * SSM_STATE
    f32 = jnp.float32
    log_dt_l = jnp.repeat(log_dt, SSM_STATE).reshape(1, n)
    b_re_t = b_re.transpose(2, 0, 1).reshape(SSM_GROUP_SIZE, n)
    b_im_t = b_im.transpose(2, 0, 1).reshape(SSM_GROUP_SIZE, n)
    return pl.pallas_call(
        _discretise_kernel,
        out_shape=(jax.ShapeDtypeStruct((1, n), f32), jax.ShapeDtypeStruct((1, n), f32),
                   jax.ShapeDtypeStruct((SSM_GROUP_SIZE, n), f32),
                   jax.ShapeDtypeStruct((SSM_GROUP_SIZE, n), f32)),
        name="s5_discretise",
    )(log_dt_l, lam_re.reshape(1, n), lam_im.reshape(1, n), b_re_t, b_im_t)


def _block_diag(blocks):
    s, g, r, c = blocks.shape
    tiled = jnp.broadcast_to(blocks[:, None], (s, g, g, r, c))
    keep = (jnp.arange(g)[:, None] == jnp.arange(g)[None, :])[None, :, :, None, None]
    placed = jnp.where(keep, tiled, 0.0)
    return placed.transpose(0, 1, 3, 2, 4).reshape(s, g * r, g * c)


def _assemble_ssm(abar_re, abar_im, bbar_re, bbar_im, c_re, c_im):
    a = jnp.concatenate([abar_re.reshape(N_SLABS, SLAB_STATE), abar_im.reshape(N_SLABS, SLAB_STATE)], axis=1)
    a = jnp.broadcast_to(a.reshape(1, STATE_LANES), (SUBLANES, STATE_LANES))

    def b_blocks(bbar):
        return bbar.reshape(SSM_GROUP_SIZE, N_SLABS, GROUPS_PER_SLAB, SSM_STATE).transpose(1, 2, 0, 3)

    b_blk = jnp.concatenate([_block_diag(b_blocks(bbar_re)), _block_diag(b_blocks(bbar_im))], axis=2)

    def c_blocks(c):
        return c.reshape(N_SLABS, GROUPS_PER_SLAB, SSM_GROUP_SIZE, SSM_STATE).transpose(0, 1, 3, 2)

    c_blk = jnp.concatenate([_block_diag(c_blocks(c_re)), _block_diag(c_blocks(-c_im))], axis=1)
    return a, b_blk.astype(jnp.bfloat16), c_blk.astype(jnp.bfloat16)


def _rmsnorm(x, g):
    return x * lax.rsqrt(jnp.mean(x * x, axis=-1, keepdims=True) + EPS) * g


def _layer_kernel(x_ref, g_ref, w_in_ref, b_in_ref, a_ref, b_blk_ref, c_blk_ref, d_ref, w_glu_ref,
                  b_glu_ref, pool_w_ref, pool_scale_ref, w_a_ref, w_b_ref, w_out_ref, fin_g_ref,
                  o_ref, bu_ref, state_ref, ub_ref, *, final_norm):
    step = pl.program_id(0)
    bf16 = jnp.bfloat16

    @pl.when(step == 0)
    def _():
        state_ref[...] = jnp.zeros_like(state_ref)
        ub_ref[0:HALO_ROWS, :] = jnp.zeros((HALO_ROWS, POOL_WIDTH), jnp.float32)

    x = x_ref[...]
    hb = _rmsnorm(x, g_ref[...]).astype(bf16)

    def proj(lo, hi):
        return _dot(hb, w_in_ref[:, lo:hi]) + b_in_ref[:, lo:hi]

    ua = proj(0, SSM_WIDTH)
    uab = ua.astype(bf16)
    for j in range(N_SLABS):
        bu_ref[:, 2 * SLAB_STATE * j:2 * SLAB_STATE * (j + 1)] = _dot(
            uab[:, LANES * j:LANES * (j + 1)], b_blk_ref[j])

    for j in range(N_SLABS):
        re_lanes = slice(2 * SLAB_STATE * j, 2 * SLAB_STATE * j + SLAB_STATE)
        im_lanes = slice(2 * SLAB_STATE * j + SLAB_STATE, 2 * SLAB_STATE * (j + 1))
        a_re = a_ref[:, re_lanes]
        a_im = a_ref[:, im_lanes]

        def scan_step(t, carry, re_lanes=re_lanes, im_lanes=im_lanes, a_re=a_re, a_im=a_im):
            s_re, s_im = carry
            rows = pl.ds(pl.multiple_of(t * BATCH, BATCH), BATCH)
            n_re = a_re * s_re - a_im * s_im + bu_ref[rows, re_lanes]
            n_im = a_re * s_im + a_im * s_re + bu_ref[rows, im_lanes]
            bu_ref[rows, re_lanes] = n_re
            bu_ref[rows, im_lanes] = n_im
            return n_re, n_im

        s_re, s_im = lax.fori_loop(0, TIME_TILE, scan_step,
                                   (state_ref[:, re_lanes], state_ref[:, im_lanes]), unroll=True)
        state_ref[:, re_lanes] = s_re
        state_ref[:, im_lanes] = s_im

    y = jnp.concatenate(
        [_dot(bu_ref[:, 2 * SLAB_STATE * j:2 * SLAB_STATE * (j + 1)].astype(bf16), c_blk_ref[j])
         for j in range(N_SLABS)], axis=1)
    y = jax.nn.gelu(y + d_ref[...] * ua)
    ya = y * jax.nn.sigmoid(_dot(y.astype(bf16), w_glu_ref[...]) + b_glu_ref[...])
    ya = ya * jax.nn.silu(proj(SSM_WIDTH, 2 * SSM_WIDTH))

    ub = proj(2 * SSM_WIDTH, 2 * SSM_WIDTH + POOL_WIDTH)
    ub_ref[HALO_ROWS:HALO_ROWS + ROWS, :] = ub
    t_abs = step * TIME_TILE + lax.broadcasted_iota(jnp.int32, (ROWS, 1), 0) // BATCH
    mixed = []
    for gi, win in enumerate(POOL_WINDOWS):
        lanes = slice(gi * POOL_GROUP, (gi + 1) * POOL_GROUP)
        tok = ub[:, lanes]
        acc = tok
        for back in range(1, win):
            lo = HALO_ROWS - back * BATCH
            acc = acc + ub_ref[lo:lo + ROWS, lanes]
        count = jnp.minimum(t_abs + 1, win).astype(jnp.float32)
        pooled = acc / count - tok
        mixed.append(_dot(pooled.astype(bf16), pool_w_ref[gi]))
    ub_ref[0:HALO_ROWS, :] = ub_ref[ROWS:ROWS + HALO_ROWS, :]
    yb = jnp.concatenate(mixed, axis=1) * pool_scale_ref[...]
    yb = yb * jax.nn.silu(proj(2 * SSM_WIDTH + POOL_WIDTH, 2 * SSM_WIDTH + 2 * POOL_WIDTH))

    ga = proj(2 * SSM_WIDTH + 2 * POOL_WIDTH, 2 * SSM_WIDTH + 2 * POOL_WIDTH + D_MODEL)
    gb = proj(2 * SSM_WIDTH + 2 * POOL_WIDTH + D_MODEL, N_IN)
    merged = (jax.nn.sigmoid(ga) * _dot(ya.astype(bf16), w_a_ref[...])
              + jax.nn.sigmoid(gb) * _dot(yb.astype(bf16), w_b_ref[...]))
    out = x + _dot(merged.astype(bf16), w_out_ref[...])
    if final_norm:
        out = _rmsnorm(out, fin_g_ref[...])
    o_ref[...] = out


def _const_spec(shape):
    zeros = (0,) * len(shape)
    return pl.BlockSpec(shape, lambda i: zeros, pipeline_mode=pl.Buffered(1))


def _layer(xt, g, w_in, b_in, a, b_blk, c_blk, d, w_glu, b_glu, pool_w, pool_scale, w_a, w_b, w_out,
           fin_g, *, final_norm):
    n_rows = xt.shape[0]
    row_spec = pl.BlockSpec((ROWS, D_MODEL), lambda i: (i, 0))
    consts = (g, w_in, b_in, a, b_blk, c_blk, d, w_glu, b_glu, pool_w, pool_scale, w_a, w_b, w_out, fin_g)
    return pl.pallas_call(
        functools.partial(_layer_kernel, final_norm=final_norm),
        grid=(n_rows // ROWS,),
        in_specs=[row_spec] + [_const_spec(c.shape) for c in consts],
        out_specs=row_spec,
        out_shape=jax.ShapeDtypeStruct((n_rows, D_MODEL), jnp.float32),
        scratch_shapes=[pltpu.VMEM((ROWS, STATE_LANES), jnp.float32),
                        pltpu.VMEM((SUBLANES, STATE_LANES), jnp.float32),
                        pltpu.VMEM((HALO_ROWS + ROWS, POOL_WIDTH), jnp.float32)],
        compiler_params=pltpu.CompilerParams(dimension_semantics=("arbitrary",),
                                             vmem_limit_bytes=VMEM_LIMIT_BYTES),
        name="hybrid_layer_final" if final_norm else "hybrid_layer",
    )(xt, *consts)


def kernel(x, norm_g, w_in, b_in, ssm_log_dt, ssm_lam_re, ssm_lam_im, ssm_b_re, ssm_b_im, ssm_c_re, ssm_c_im, ssm_d, ssm_w_glu, ssm_b_glu, pool_w, pool_scale, w_branch_a, w_branch_b, w_out, final_norm_g):
    bf16 = jnp.bfloat16
    xt = x.transpose(1, 0, 2).reshape(SEQ * BATCH, D_MODEL)
    for l in range(DEPTH):
        disc = _discretise(ssm_log_dt[l], ssm_lam_re[l], ssm_lam_im[l], ssm_b_re[l], ssm_b_im[l])
        a, b_blk, c_blk = _assemble_ssm(*disc, ssm_c_re[l], ssm_c_im[l])
        xt = _layer(xt, norm_g[l].reshape(1, D_MODEL), w_in[l].astype(bf16), b_in[l].reshape(1, N_IN),
                    a, b_blk, c_blk, ssm_d[l].reshape(1, SSM_WIDTH), ssm_w_glu[l].astype(bf16),
                    ssm_b_glu[l].reshape(1, SSM_WIDTH), pool_w[l].astype(bf16),
                    pool_scale[l].reshape(1, POOL_WIDTH), w_branch_a[l].astype(bf16),
                    w_branch_b[l].astype(bf16), w_out[l].astype(bf16),
                    final_norm_g.reshape(1, D_MODEL), final_norm=(l == DEPTH - 1))
    return xt.reshape(SEQ, BATCH, D_MODEL).transpose(1, 0, 2)
```

```python
import functools

import jax
import jax.numpy as jnp
from jax import lax
from jax.experimental import pallas as pl
from jax.experimental.pallas import tpu as pltpu

D_MODEL = 1024
BATCH = 8
SEQ = 2048
DEPTH = 2
EPS = 1e-6
SSM_WIDTH = D_MODEL // 2
SSM_GROUP_SIZE = 16
SSM_GROUPS = SSM_WIDTH // SSM_GROUP_SIZE
SSM_STATE = 64
POOL_WIDTH = D_MODEL // 2
POOL_WINDOWS = (2, 4, 8, 16)
POOL_GROUP = POOL_WIDTH // len(POOL_WINDOWS)
N_IN = 2 * SSM_WIDTH + 2 * POOL_WIDTH + 2 * D_MODEL

SUBLANES = 8
LANES = 128
assert BATCH == SUBLANES

GROUPS_PER_SLAB = LANES // SSM_GROUP_SIZE
N_SLABS = SSM_GROUPS // GROUPS_PER_SLAB
SLAB_STATE = GROUPS_PER_SLAB * SSM_STATE
STATE_LANES = 2 * SSM_GROUPS * SSM_STATE

TIME_TILE = 64
ROWS = TIME_TILE * BATCH
HALO_STEPS = 16
HALO_ROWS = HALO_STEPS * BATCH
VMEM_LIMIT_BYTES = 48 * 1024 * 1024


def _dot(a, b):
    return jnp.dot(a, b, preferred_element_type=jnp.float32)


def _discretise_kernel(log_dt_ref, lam_re_ref, lam_im_ref, b_re_ref, b_im_ref,
                       abar_re_ref, abar_im_ref, bbar_re_ref, bbar_im_ref):
    dt = jnp.exp(log_dt_ref[...])
    lam_re = lam_re_ref[...]
    lam_im = lam_im_ref[...]
    mag = jnp.exp(lam_re * dt)
    ang = lam_im * dt
    abar_re = mag * jnp.cos(ang)
    abar_im = mag * jnp.sin(ang)
    num_re = abar_re - 1.0
    num_im = abar_im
    den = lam_re * lam_re + lam_im * lam_im
    coef_re = (num_re * lam_re + num_im * lam_im) / den
    coef_im = (num_im * lam_re - num_re * lam_im) / den
    b_re = b_re_ref[...]
    b_im = b_im_ref[...]
    abar_re_ref[...] = abar_re
    abar_im_ref[...] = abar_im
    bbar_re_ref[...] = coef_re * b_re - coef_im * b_im
    bbar_im_ref[...] = coef_re * b_im + coef_im * b_re


def _discretise(log_dt, lam_re, lam_im, b_re, b_im):
    n = SSM_GROUPS * SSM_STATE
    f32 = jnp.float32
    log_dt_l = jnp.repeat(log_dt, SSM_STATE).reshape(1, n)
    b_re_t = b_re.transpose(2, 0, 1).reshape(SSM_GROUP_SIZE, n)
    b_im_t = b_im.transpose(2, 0, 1).reshape(SSM_GROUP_SIZE, n)
    return pl.pallas_call(
        _discretise_kernel,
        out_shape=(jax.ShapeDtypeStruct((1, n), f32), jax.ShapeDtypeStruct((1, n), f32),
                   jax.ShapeDtypeStruct((SSM_GROUP_SIZE, n), f32),
                   jax.ShapeDtypeStruct((SSM_GROUP_SIZE, n), f32)),
        name="s5_discretise",
    )(log_dt_l, lam_re.reshape(1, n), lam_im.reshape(1, n), b_re_t, b_im_t)


def _block_diag(blocks):
    s, g, r, c = blocks.shape
    tiled = jnp.broadcast_to(blocks[:, None], (s, g, g, r, c))
    keep = (jnp.arange(g)[:, None] == jnp.arange(g)[None, :])[None, :, :, None, None]
    placed = jnp.where(keep, tiled, 0.0)
    return placed.transpose(0, 1, 3, 2, 4).reshape(s, g * r, g * c)


def _assemble_ssm(abar_re, abar_im, bbar_re, bbar_im, c_re, c_im):
    a = jnp.concatenate([abar_re.reshape(N_SLABS, SLAB_STATE), abar_im.reshape(N_SLABS, SLAB_STATE)], axis=1)
    a = jnp.broadcast_to(a.reshape(1, STATE_LANES), (SUBLANES, STATE_LANES))

    def b_blocks(bbar):
        return bbar.reshape(SSM_GROUP_SIZE, N_SLABS, GROUPS_PER_SLAB, SSM_STATE).transpose(1, 2, 0, 3)

    b_blk = jnp.concatenate([_block_diag(b_blocks(bbar_re)), _block_diag(b_blocks(bbar_im))], axis=2)

    def c_blocks(c):
        return c.reshape(N_SLABS, GROUPS_PER_SLAB, SSM_GROUP_SIZE, SSM_STATE).transpose(0, 1, 3, 2)

    c_blk = jnp.concatenate([_block_diag(c_blocks(c_re)), _block_diag(c_blocks(-c_im))], axis=1)
    return a, b_blk.astype(jnp.bfloat16), c_blk.astype(jnp.bfloat16)


def _rmsnorm(x, g):
    return x * lax.rsqrt(jnp.mean(x * x, axis=-1, keepdims=True) + EPS) * g


def _layer_kernel(x_ref, g_ref, w_in_ref, b_in_ref, a_ref, b_blk_ref, c_blk_ref, d_ref, w_glu_ref,
                  b_glu_ref, pool_w_ref, pool_scale_ref, w_a_ref, w_b_ref, w_out_ref, fin_g_ref,
                  o_ref, bu_ref, state_ref, ub_ref, *, first, last):
    step = pl.program_id(0)
    bf16 = jnp.bfloat16

    @pl.when(step == 0)
    def _():
        state_ref[...] = jnp.zeros_like(state_ref)
        ub_ref[0:HALO_ROWS, :] = jnp.zeros((HALO_ROWS, POOL_WIDTH), jnp.float32)

    if first:
        x = jnp.swapaxes(x_ref[...], 0, 1).reshape(ROWS, D_MODEL)
    else:
        x = x_ref[...]
    hb = _rmsnorm(x, g_ref[...]).astype(bf16)

    def proj(lo, hi):
        return _dot(hb, w_in_ref[:, lo:hi]) + b_in_ref[:, lo:hi]

    ua = proj(0, SSM_WIDTH)
    uab = ua.astype(bf16)
    for j in range(N_SLABS):
        bu_ref[:, 2 * SLAB_STATE * j:2 * SLAB_STATE * (j + 1)] = _dot(
            uab[:, LANES * j:LANES * (j + 1)], b_blk_ref[j])

    for j in range(N_SLABS):
        re_lanes = slice(2 * SLAB_STATE * j, 2 * SLAB_STATE * j + SLAB_STATE)
        im_lanes = slice(2 * SLAB_STATE * j + SLAB_STATE, 2 * SLAB_STATE * (j + 1))
        a_re = a_ref[:, re_lanes]
        a_im = a_ref[:, im_lanes]

        def scan_step(t, carry, re_lanes=re_lanes, im_lanes=im_lanes, a_re=a_re, a_im=a_im):
            s_re, s_im = carry
            rows = pl.ds(pl.multiple_of(t * BATCH, BATCH), BATCH)
            n_re = a_re * s_re - a_im * s_im + bu_ref[rows, re_lanes]
            n_im = a_re * s_im + a_im * s_re + bu_ref[rows, im_lanes]
            bu_ref[rows, re_lanes] = n_re
            bu_ref[rows, im_lanes] = n_im
            return n_re, n_im

        s_re, s_im = lax.fori_loop(0, TIME_TILE, scan_step,
                                   (state_ref[:, re_lanes], state_ref[:, im_lanes]), unroll=True)
        state_ref[:, re_lanes] = s_re
        state_ref[:, im_lanes] = s_im

    y = jnp.concatenate(
        [_dot(bu_ref[:, 2 * SLAB_STATE * j:2 * SLAB_STATE * (j + 1)].astype(bf16), c_blk_ref[j])
         for j in range(N_SLABS)], axis=1)
    y = jax.nn.gelu(y + d_ref[...] * ua)
    ya = y * jax.nn.sigmoid(_dot(y.astype(bf16), w_glu_ref[...]) + b_glu_ref[...])
    ya = ya * jax.nn.silu(proj(SSM_WIDTH, 2 * SSM_WIDTH))

    ub = proj(2 * SSM_WIDTH, 2 * SSM_WIDTH + POOL_WIDTH)
    ub_ref[HALO_ROWS:HALO_ROWS + ROWS, :] = ub
    t_abs = step * TIME_TILE + lax.broadcasted_iota(jnp.int32, (ROWS, 1), 0) // BATCH
    mixed = []
    for gi, win in enumerate(POOL_WINDOWS):
        lanes = slice(gi * POOL_GROUP, (gi + 1) * POOL_GROUP)
        tok = ub[:, lanes]
        acc = tok
        for back in range(1, win):
            lo = HALO_ROWS - back * BATCH
            acc = acc + ub_ref[lo:lo + ROWS, lanes]
        count = jnp.minimum(t_abs + 1, win).astype(jnp.float32)
        pooled = acc / count - tok
        mixed.append(_dot(pooled.astype(bf16), pool_w_ref[gi]))
    ub_ref[0:HALO_ROWS, :] = ub_ref[ROWS:ROWS + HALO_ROWS, :]
    yb = jnp.concatenate(mixed, axis=1) * pool_scale_ref[...]
    yb = yb * jax.nn.silu(proj(2 * SSM_WIDTH + POOL_WIDTH, 2 * SSM_WIDTH + 2 * POOL_WIDTH))

    ga = proj(2 * SSM_WIDTH + 2 * POOL_WIDTH, 2 * SSM_WIDTH + 2 * POOL_WIDTH + D_MODEL)
    gb = proj(2 * SSM_WIDTH + 2 * POOL_WIDTH + D_MODEL, N_IN)
    merged = (jax.nn.sigmoid(ga) * _dot(ya.astype(bf16), w_a_ref[...])
              + jax.nn.sigmoid(gb) * _dot(yb.astype(bf16), w_b_ref[...]))
    out = x + _dot(merged.astype(bf16), w_out_ref[...])
    if last:
        out = _rmsnorm(out, fin_g_ref[...])
        o_ref[...] = jnp.swapaxes(out.reshape(TIME_TILE, BATCH, D_MODEL), 0, 1)
    else:
        o_ref[...] = out


def _const_spec(shape):
    zeros = (0,) * len(shape)
    return pl.BlockSpec(shape, lambda i: zeros, pipeline_mode=pl.Buffered(1))


def _layer(xt, g, w_in, b_in, a, b_blk, c_blk, d, w_glu, b_glu, pool_w, pool_scale, w_a, w_b, w_out,
           fin_g, *, first, last):
    time_major = pl.BlockSpec((ROWS, D_MODEL), lambda i: (i, 0))
    batch_major = pl.BlockSpec((BATCH, TIME_TILE, D_MODEL), lambda i: (0, i, 0))
    consts = (g, w_in, b_in, a, b_blk, c_blk, d, w_glu, b_glu, pool_w, pool_scale, w_a, w_b, w_out, fin_g)
    return pl.pallas_call(
        functools.partial(_layer_kernel, first=first, last=last),
        grid=(SEQ // TIME_TILE,),
        in_specs=[batch_major if first else time_major] + [_const_spec(c.shape) for c in consts],
        out_specs=batch_major if last else time_major,
        out_shape=jax.ShapeDtypeStruct((BATCH, SEQ, D_MODEL) if last else (SEQ * BATCH, D_MODEL),
                                       jnp.float32),
        scratch_shapes=[pltpu.VMEM((ROWS, STATE_LANES), jnp.float32),
                        pltpu.VMEM((SUBLANES, STATE_LANES), jnp.float32),
                        pltpu.VMEM((HALO_ROWS + ROWS, POOL_WIDTH), jnp.float32)],
        compiler_params=pltpu.CompilerParams(dimension_semantics=("arbitrary",),
                                             vmem_limit_bytes=VMEM_LIMIT_BYTES),
        name=f"hybrid_layer_{int(first)}{int(last)}",
    )(xt, *consts)


def kernel(x, norm_g, w_in, b_in, ssm_log_dt, ssm_lam_re, ssm_lam_im, ssm_b_re, ssm_b_im, ssm_c_re, ssm_c_im, ssm_d, ssm_w_glu, ssm_b_glu, pool_w, pool_scale, w_branch_a, w_branch_b, w_out, final_norm_g):
    bf16 = jnp.bfloat16
    xt = x
    for l in range(DEPTH):
        disc = _discretise(ssm_log_dt[l], ssm_lam_re[l], ssm_lam_im[l], ssm_b_re[l], ssm_b_im[l])
        a, b_blk, c_blk = _assemble_ssm(*disc, ssm_c_re[l], ssm_c_im[l])
        xt = _layer(xt, norm_g[l].reshape(1, D_MODEL), w_in[l].astype(bf16), b_in[l].reshape(1, N_IN),
                    a, b_blk, c_blk, ssm_d[l].reshape(1, SSM_WIDTH), ssm_w_glu[l].astype(bf16),
                    ssm_b_glu[l].reshape(1, SSM_WIDTH), pool_w[l].astype(bf16),
                    pool_scale[l].reshape(1, POOL_WIDTH), w_branch_a[l].astype(bf16),
                    w_branch_b[l].astype(bf16), w_out[l].astype(bf16),
                    final_norm_g.reshape(1, D_MODEL), first=(l == 0), last=(l == DEPTH - 1))
    return xt
```

```python
import functools

import jax
import jax.numpy as jnp
from jax import lax
from jax.experimental import pallas as pl
from jax.experimental.pallas import tpu as pltpu

D_MODEL = 1024
BATCH = 8
SEQ = 2048
DEPTH = 2
EPS = 1e-6
SSM_WIDTH = D_MODEL // 2
SSM_GROUP_SIZE = 16
SSM_GROUPS = SSM_WIDTH // SSM_GROUP_SIZE
SSM_STATE = 64
POOL_WIDTH = D_MODEL // 2
POOL_WINDOWS = (2, 4, 8, 16)
POOL_GROUP = POOL_WIDTH // len(POOL_WINDOWS)
N_IN = 2 * SSM_WIDTH + 2 * POOL_WIDTH + 2 * D_MODEL

SUBLANES = 8
LANES = 128
assert BATCH == SUBLANES

GROUPS_PER_SLAB = LANES // SSM_GROUP_SIZE
N_SLABS = SSM_GROUPS // GROUPS_PER_SLAB
SLAB_STATE = GROUPS_PER_SLAB * SSM_STATE
SLAB_LANES = 2 * SLAB_STATE
STATE_LANES = N_SLABS * SLAB_LANES
PAIR = 2
WB_ROWS = PAIR * LANES
WC_ROWS = SLAB_LANES + PAIR * LANES
POOL_PAIRS = len(POOL_WINDOWS) // 2

TIME_TILE = 64
ROWS = TIME_TILE * BATCH
PAIRS = TIME_TILE // PAIR
PAIR_ROWS = PAIRS * BATCH
HALO_STEPS = 16
HALO_ROWS = HALO_STEPS * BATCH
VMEM_LIMIT_BYTES = 52 * 1024 * 1024


def _dot(a, b):
    return jnp.dot(a, b, preferred_element_type=jnp.float32)


def _dot_f32(a, b):
    return jnp.dot(a, b, preferred_element_type=jnp.float32, precision=lax.Precision.HIGHEST)


def _cmul(ar, ai, br, bi):
    return ar * br - ai * bi, ar * bi + ai * br


def _zoh(log_dt, lam_re, lam_im):
    dt = jnp.exp(log_dt)
    mag = jnp.exp(lam_re * dt)
    ang = lam_im * dt
    a_re = mag * jnp.cos(ang)
    a_im = mag * jnp.sin(ang)
    num_re = a_re - 1.0
    num_im = a_im
    den = lam_re * lam_re + lam_im * lam_im
    coef_re = (num_re * lam_re + num_im * lam_im) / den
    coef_im = (num_im * lam_re - num_re * lam_im) / den
    return a_re, a_im, coef_re, coef_im


def _prepare_kernel(ldt_r_ref, lre_r_ref, lim_r_ref, ldt_c_ref, lre_c_ref, lim_c_ref,
                    b_re_ref, b_im_ref, c_re_ref, c_im_ref, a2_ref, wb_ref, wc_ref):
    f32 = jnp.float32
    a_re, a_im, coef_re, coef_im = _zoh(ldt_r_ref[...], lre_r_ref[...], lim_r_ref[...])
    a2_re, a2_im = _cmul(a_re, a_im, a_re, a_im)
    a2_ref[...] = jnp.broadcast_to(jnp.concatenate([a2_re, a2_im], axis=1), (SUBLANES, SLAB_LANES))

    row_g = lax.broadcasted_iota(jnp.int32, (LANES, SLAB_STATE), 0) // SSM_GROUP_SIZE
    col_g = lax.broadcasted_iota(jnp.int32, (LANES, SLAB_STATE), 1) // SSM_STATE
    keep_b = row_g == col_g
    bbar_re, bbar_im = _cmul(coef_re, coef_im, b_re_ref[...], b_im_ref[...])
    bbar_re = jnp.where(keep_b, bbar_re, 0.0)
    bbar_im = jnp.where(keep_b, bbar_im, 0.0)
    abbar_re, abbar_im = _cmul(a_re, a_im, bbar_re, bbar_im)
    wb_ref[0:LANES, :] = jnp.concatenate([abbar_re, abbar_im], axis=1).astype(wb_ref.dtype)
    wb_ref[LANES:WB_ROWS, :] = jnp.concatenate([bbar_re, bbar_im], axis=1).astype(wb_ref.dtype)

    ac_re, ac_im, _, _ = _zoh(ldt_c_ref[...], lre_c_ref[...], lim_c_ref[...])
    a2c_re, a2c_im = _cmul(ac_re, ac_im, ac_re, ac_im)
    row_g = lax.broadcasted_iota(jnp.int32, (SLAB_STATE, LANES), 0) // SSM_STATE
    col_g = lax.broadcasted_iota(jnp.int32, (SLAB_STATE, LANES), 1) // SSM_GROUP_SIZE
    keep_c = row_g == col_g
    c_re = jnp.where(keep_c, c_re_ref[...], 0.0)
    c_im = jnp.where(keep_c, c_im_ref[...], 0.0)
    ca_re, ca_im = _cmul(ac_re, ac_im, c_re, c_im)
    ca2_re, ca2_im = _cmul(a2c_re, a2c_im, c_re, c_im)
    wc_ref[0:SLAB_STATE, :] = jnp.concatenate([ca_re, ca2_re], axis=1).astype(wc_ref.dtype)
    wc_ref[SLAB_STATE:SLAB_LANES, :] = jnp.concatenate([-ca_im, -ca2_im], axis=1).astype(wc_ref.dtype)
    k0 = _dot_f32(bbar_re, c_re) - _dot_f32(bbar_im, c_im)
    k1 = _dot_f32(bbar_re, ca_re) - _dot_f32(bbar_im, ca_im)
    wc_ref[SLAB_LANES:SLAB_LANES + LANES, :] = jnp.concatenate([k0, k1], axis=1).astype(wc_ref.dtype)
    wc_ref[SLAB_LANES + LANES:WC_ROWS, :] = jnp.concatenate(
        [jnp.zeros((LANES, LANES), f32), k0], axis=1).astype(wc_ref.dtype)


def _prepare(log_dt, lam_re, lam_im, b_re, b_im, c_re, c_im):
    f32 = jnp.float32
    depth = log_dt.shape[0]
    slabbed = (depth, N_SLABS, GROUPS_PER_SLAB, SSM_STATE)
    ldt = jnp.broadcast_to(log_dt.reshape(depth, N_SLABS, GROUPS_PER_SLAB, 1), slabbed)

    def rows(v):
        return v.reshape(depth, N_SLABS, 1, SLAB_STATE)

    def cols(v):
        return v.reshape(depth, N_SLABS, SLAB_STATE, 1)

    def b_tiled(b):
        t = b.reshape(depth, N_SLABS, GROUPS_PER_SLAB, SSM_STATE, SSM_GROUP_SIZE).transpose(0, 1, 4, 2, 3)
        t = t.reshape(depth, N_SLABS, 1, SSM_GROUP_SIZE, SLAB_STATE)
        t = jnp.broadcast_to(t, (depth, N_SLABS, GROUPS_PER_SLAB, SSM_GROUP_SIZE, SLAB_STATE))
        return t.reshape(depth, N_SLABS, LANES, SLAB_STATE)

    def c_tiled(c):
        t = c.reshape(depth, N_SLABS, GROUPS_PER_SLAB, SSM_GROUP_SIZE, SSM_STATE).transpose(0, 1, 2, 4, 3)
        t = t.reshape(depth, N_SLABS, SLAB_STATE, 1, SSM_GROUP_SIZE)
        t = jnp.broadcast_to(t, (depth, N_SLABS, SLAB_STATE, GROUPS_PER_SLAB, SSM_GROUP_SIZE))
        return t.reshape(depth, N_SLABS, SLAB_STATE, LANES)

    def spec(r, c):
        return pl.BlockSpec((None, None, r, c), lambda l, j: (l, j, 0, 0))

    row_spec, col_spec = spec(1, SLAB_STATE), spec(SLAB_STATE, 1)
    return pl.pallas_call(
        _prepare_kernel,
        grid=(depth, N_SLABS),
        in_specs=[row_spec] * 3 + [col_spec] * 3 + [spec(LANES, SLAB_STATE)] * 2 + [spec(SLAB_STATE, LANES)] * 2,
        out_specs=(pl.BlockSpec((None, SUBLANES, SLAB_LANES), lambda l, j: (l, 0, j)),
                   spec(WB_ROWS, SLAB_LANES), spec(WC_ROWS, PAIR * LANES)),
        out_shape=(jax.ShapeDtypeStruct((depth, SUBLANES, STATE_LANES), f32),
                   jax.ShapeDtypeStruct((depth, N_SLABS, WB_ROWS, SLAB_LANES), jnp.bfloat16),
                   jax.ShapeDtypeStruct((depth, N_SLABS, WC_ROWS, PAIR * LANES), jnp.bfloat16)),
        name="s5_prepare",
    )(rows(ldt), rows(lam_re), rows(lam_im), cols(ldt), cols(lam_re), cols(lam_im),
      b_tiled(b_re), b_tiled(b_im), c_tiled(c_re), c_tiled(c_im))


def _rmsnorm(x, g):
    return x * lax.rsqrt(jnp.mean(x * x, axis=-1, keepdims=True) + EPS) * g


def _split_pairs(v):
    v = v.reshape(PAIRS, PAIR, BATCH, v.shape[-1])
    return v[:, 0].reshape(PAIR_ROWS, -1), v[:, 1].reshape(PAIR_ROWS, -1)


def _layer_kernel(x_ref, g_ref, w_in_ref, b_in_ref, a2_ref, wb_ref, wc_ref, d_ref, w_glu_ref,
                  b_glu_ref, pool_w_ref, pool_scale_ref, w_a_ref, w_b_ref, w_out_ref, fin_g_ref,
                  o_ref, s_ref, y_ref, state_ref, ub_ref, *, first, last):
    step = pl.program_id(0)
    bf16 = jnp.bfloat16

    @pl.when(step == 0)
    def _():
        state_ref[...] = jnp.zeros_like(state_ref)
        ub_ref[0:HALO_ROWS, :] = jnp.zeros((HALO_ROWS, POOL_WIDTH), jnp.float32)

    if first:
        x = jnp.swapaxes(x_ref[...], 0, 1).reshape(ROWS, D_MODEL)
    else:
        x = x_ref[...]
    hb = _rmsnorm(x, g_ref[...]).astype(bf16)

    def proj(lo, hi):
        return _dot(hb, w_in_ref[:, lo:hi]) + b_in_ref[:, lo:hi]

    ua = proj(0, SSM_WIDTH)
    u_even, u_odd = _split_pairs(ua)
    uu = [jnp.concatenate([u_even[:, LANES * j:LANES * (j + 1)], u_odd[:, LANES * j:LANES * (j + 1)]],
                          axis=1).astype(bf16) for j in range(N_SLABS)]
    for j in range(N_SLABS):
        s_ref[:, SLAB_LANES * j:SLAB_LANES * (j + 1)] = _dot(uu[j], wb_ref[j])

    for j in range(N_SLABS):
        re_lanes = slice(SLAB_LANES * j, SLAB_LANES * j + SLAB_STATE)
        im_lanes = slice(SLAB_LANES * j + SLAB_STATE, SLAB_LANES * (j + 1))
        a_re = a2_ref[:, re_lanes]
        a_im = a2_ref[:, im_lanes]

        def scan_step(q, carry, re_lanes=re_lanes, im_lanes=im_lanes, a_re=a_re, a_im=a_im):
            s_re, s_im = carry
            rows = pl.ds(pl.multiple_of(q * BATCH, BATCH), BATCH)
            n_re = a_re * s_re - a_im * s_im + s_ref[rows, re_lanes]
            n_im = a_re * s_im + a_im * s_re + s_ref[rows, im_lanes]
            s_ref[rows, re_lanes] = s_re
            s_ref[rows, im_lanes] = s_im
            return n_re, n_im

        s_re, s_im = lax.fori_loop(0, PAIRS, scan_step,
                                   (state_ref[:, re_lanes], state_ref[:, im_lanes]), unroll=True)
        state_ref[:, re_lanes] = s_re
        state_ref[:, im_lanes] = s_im

    for j in range(N_SLABS):
        yy = (_dot(s_ref[:, SLAB_LANES * j:SLAB_LANES * (j + 1)].astype(bf16), wc_ref[j, 0:SLAB_LANES, :])
              + _dot(uu[j], wc_ref[j, SLAB_LANES:WC_ROWS, :]))
        lanes = slice(LANES * j, LANES * (j + 1))
        y_ref[:, 0:BATCH, lanes] = yy[:, 0:LANES].reshape(PAIRS, BATCH, LANES)
        y_ref[:, BATCH:PAIR * BATCH, lanes] = yy[:, LANES:PAIR * LANES].reshape(PAIRS, BATCH, LANES)
    y = y_ref[...].reshape(ROWS, SSM_WIDTH)
    y = jax.nn.gelu(y + d_ref[...] * ua)
    ya = y * jax.nn.sigmoid(_dot(y.astype(bf16), w_glu_ref[...]) + b_glu_ref[...])
    ya = ya * jax.nn.silu(proj(SSM_WIDTH, 2 * SSM_WIDTH))

    ub = proj(2 * SSM_WIDTH, 2 * SSM_WIDTH + POOL_WIDTH)
    ub_ref[HALO_ROWS:HALO_ROWS + ROWS, :] = ub
    t_abs = step * TIME_TILE + lax.broadcasted_iota(jnp.int32, (ROWS, 1), 0) // BATCH
    pooled = []
    for gi, win in enumerate(POOL_WINDOWS):
        lanes = slice(gi * POOL_GROUP, (gi + 1) * POOL_GROUP)
        tok = ub[:, lanes]
        acc = tok
        for back in range(1, win):
            lo = HALO_ROWS - back * BATCH
            acc = acc + ub_ref[lo:lo + ROWS, lanes]
        count = jnp.minimum(t_abs + 1, win).astype(jnp.float32)
        pooled.append((acc / count - tok).astype(bf16))
    ub_ref[0:HALO_ROWS, :] = ub_ref[ROWS:ROWS + HALO_ROWS, :]
    mixed = [_dot(jnp.concatenate(pooled[2 * i:2 * i + 2], axis=1), pool_w_ref[i]) for i in range(POOL_PAIRS)]
    yb = jnp.concatenate(mixed, axis=1) * pool_scale_ref[...]
    yb = yb * jax.nn.silu(proj(2 * SSM_WIDTH + POOL_WIDTH, 2 * SSM_WIDTH + 2 * POOL_WIDTH))

    ga = proj(2 * SSM_WIDTH + 2 * POOL_WIDTH, 2 * SSM_WIDTH + 2 * POOL_WIDTH + D_MODEL)
    gb = proj(2 * SSM_WIDTH + 2 * POOL_WIDTH + D_MODEL, N_IN)
    merged = (jax.nn.sigmoid(ga) * _dot(ya.astype(bf16), w_a_ref[...])
              + jax.nn.sigmoid(gb) * _dot(yb.astype(bf16), w_b_ref[...]))
    out = x + _dot(merged.astype(bf16), w_out_ref[...])
    if last:
        out = _rmsnorm(out, fin_g_ref[...])
        o_ref[...] = jnp.swapaxes(out.reshape(TIME_TILE, BATCH, D_MODEL), 0, 1)
    else:
        o_ref[...] = out


def _layer_spec(arr, layer):
    zeros = (0,) * (arr.ndim - 1)
    return pl.BlockSpec((None,) + arr.shape[1:], lambda i: (layer,) + zeros, pipeline_mode=pl.Buffered(1))


def _layer(xt, layer, per_layer, fin_g, *, first, last):
    time_major = pl.BlockSpec((ROWS, D_MODEL), lambda i: (i, 0))
    batch_major = pl.BlockSpec((BATCH, TIME_TILE, D_MODEL), lambda i: (0, i, 0))
    fin_spec = pl.BlockSpec(fin_g.shape, lambda i: (0, 0), pipeline_mode=pl.Buffered(1))
    return pl.pallas_call(
        functools.partial(_layer_kernel, first=first, last=last),
        grid=(SEQ // TIME_TILE,),
        in_specs=([batch_major if first else time_major] + [_layer_spec(p, layer) for p in per_layer]
                  + [fin_spec]),
        out_specs=batch_major if last else time_major,
        out_shape=jax.ShapeDtypeStruct((BATCH, SEQ, D_MODEL) if last else (SEQ * BATCH, D_MODEL),
                                       jnp.float32),
        scratch_shapes=[pltpu.VMEM((PAIR_ROWS, STATE_LANES), jnp.float32),
                        pltpu.VMEM((PAIRS, PAIR * BATCH, SSM_WIDTH), jnp.float32),
                        pltpu.VMEM((SUBLANES, STATE_LANES), jnp.float32),
                        pltpu.VMEM((HALO_ROWS + ROWS, POOL_WIDTH), jnp.float32)],
        compiler_params=pltpu.CompilerParams(dimension_semantics=("arbitrary",),
                                             vmem_limit_bytes=VMEM_LIMIT_BYTES),
        name=f"hybrid_layer_{int(first)}{int(last)}",
    )(xt, *per_layer, fin_g)


def _pool_pairs(pool_w):
    depth = pool_w.shape[0]
    w = pool_w.reshape(depth, POOL_PAIRS, 2, 1, POOL_GROUP, POOL_GROUP)
    w = jnp.broadcast_to(w, (depth, POOL_PAIRS, 2, 2, POOL_GROUP, POOL_GROUP))
    keep = (jnp.arange(2)[:, None] == jnp.arange(2)[None, :])[None, None, :, :, None, None]
    w = jnp.where(keep, w, 0.0).transpose(0, 1, 2, 4, 3, 5)
    return w.reshape(depth, POOL_PAIRS, 2 * POOL_GROUP, 2 * POOL_GROUP)


def kernel(x, norm_g, w_in, b_in, ssm_log_dt, ssm_lam_re, ssm_lam_im, ssm_b_re, ssm_b_im, ssm_c_re, ssm_c_im, ssm_d, ssm_w_glu, ssm_b_glu, pool_w, pool_scale, w_branch_a, w_branch_b, w_out, final_norm_g):
    bf16 = jnp.bfloat16
    a2, wb, wc = _prepare(ssm_log_dt, ssm_lam_re, ssm_lam_im, ssm_b_re, ssm_b_im, ssm_c_re, ssm_c_im)

    def vec(v):
        return v.reshape(DEPTH, 1, v.shape[-1])

    per_layer = (vec(norm_g), w_in.astype(bf16), vec(b_in), a2, wb, wc, vec(ssm_d), ssm_w_glu.astype(bf16),
                 vec(ssm_b_glu), _pool_pairs(pool_w).astype(bf16), vec(pool_scale), w_branch_a.astype(bf16),
                 w_branch_b.astype(bf16), w_out.astype(bf16))
    fin_g = final_norm_g.reshape(1, D_MODEL)
    xt = x
    for l in range(DEPTH):
        xt = _layer(xt, l, per_layer, fin_g, first=(l == 0), last=(l == DEPTH - 1))
    return xt
```

```python
import functools

import jax
import jax.numpy as jnp
from jax import lax
from jax.experimental import pallas as pl
from jax.experimental.pallas import tpu as pltpu

D_MODEL = 1024
BATCH = 8
SEQ = 2048
DEPTH = 2
EPS = 1e-6
SSM_WIDTH = D_MODEL // 2
SSM_GROUP_SIZE = 16
SSM_GROUPS = SSM_WIDTH // SSM_GROUP_SIZE
SSM_STATE = 64
POOL_WIDTH = D_MODEL // 2
POOL_WINDOWS = (2, 4, 8, 16)
POOL_GROUP = POOL_WIDTH // len(POOL_WINDOWS)
N_IN = 2 * SSM_WIDTH + 2 * POOL_WIDTH + 2 * D_MODEL

SUBLANES = 8
LANES = 128
assert BATCH == SUBLANES

GROUPS_PER_SLAB = LANES // SSM_GROUP_SIZE
N_SLABS = SSM_GROUPS // GROUPS_PER_SLAB
SLAB_STATE = GROUPS_PER_SLAB * SSM_STATE
SLAB_LANES = 2 * SLAB_STATE
STATE_LANES = N_SLABS * SLAB_LANES
PAIR = 2
WB_ROWS = PAIR * LANES
WC_ROWS = SLAB_LANES + PAIR * LANES
POOL_PAIRS = len(POOL_WINDOWS) // 2

TIME_TILE = 64
ROWS = TIME_TILE * BATCH
N_SUB = 2
SUB_STEPS = TIME_TILE // N_SUB
SUB_ROWS = SUB_STEPS * BATCH
SUB_PAIRS = SUB_STEPS // PAIR
PAIRS = TIME_TILE // PAIR
HALO_STEPS = 16
HALO_ROWS = HALO_STEPS * BATCH
VMEM_LIMIT_BYTES = 52 * 1024 * 1024


_dot = functools.partial(jnp.dot, preferred_element_type=jnp.float32)


def _cmul(ar, ai, br, bi):
    return ar * br - ai * bi, ar * bi + ai * br


def _zoh(log_dt, lam_re, lam_im):
    dt = jnp.exp(log_dt)
    mag = jnp.exp(lam_re * dt)
    ang = lam_im * dt
    a_re = mag * jnp.cos(ang)
    a_im = mag * jnp.sin(ang)
    num_re = a_re - 1.0
    num_im = a_im
    den = lam_re * lam_re + lam_im * lam_im
    coef_re = (num_re * lam_re + num_im * lam_im) / den
    coef_im = (num_im * lam_re - num_re * lam_im) / den
    return a_re, a_im, coef_re, coef_im


def _prepare_kernel(ldt_ref, lre_ref, lim_ref, b_re_ref, b_im_ref, c_re_ref, c_im_ref, a2_ref, wb_ref, wc_ref):
    row_g = lax.broadcasted_iota(jnp.int32, (LANES, SLAB_STATE), 0) // SSM_GROUP_SIZE
    col_g = lax.broadcasted_iota(jnp.int32, (LANES, SLAB_STATE), 1) // SSM_STATE
    keep = row_g == col_g
    nt = (((1,), (1,)), ((), ()))
    for j in range(N_SLABS):
        a_re, a_im, coef_re, coef_im = _zoh(ldt_ref[j], lre_ref[j], lim_ref[j])
        a2_re, a2_im = _cmul(a_re, a_im, a_re, a_im)
        a2_ref[:, SLAB_LANES * j:SLAB_LANES * (j + 1)] = jnp.broadcast_to(
            jnp.concatenate([a2_re, a2_im], axis=1), (SUBLANES, SLAB_LANES))

        bbar_re, bbar_im = _cmul(coef_re, coef_im, b_re_ref[j], b_im_ref[j])
        bbar_re = jnp.where(keep, bbar_re, 0.0)
        bbar_im = jnp.where(keep, bbar_im, 0.0)
        abbar_re, abbar_im = _cmul(a_re, a_im, bbar_re, bbar_im)
        wb_ref[j, 0:LANES, :] = jnp.concatenate([abbar_re, abbar_im], axis=1).astype(wb_ref.dtype)
        wb_ref[j, LANES:WB_ROWS, :] = jnp.concatenate([bbar_re, bbar_im], axis=1).astype(wb_ref.dtype)

        c_re = jnp.where(keep, c_re_ref[j], 0.0)
        c_im = jnp.where(keep, c_im_ref[j], 0.0)
        ca_re, ca_im = _cmul(a_re, a_im, c_re, c_im)
        ca2_re, ca2_im = _cmul(a2_re, a2_im, c_re, c_im)
        wc_ref[j, 0:SLAB_STATE, :] = jnp.concatenate([ca_re, ca2_re], axis=0).T.astype(wc_ref.dtype)
        wc_ref[j, SLAB_STATE:SLAB_LANES, :] = jnp.concatenate([-ca_im, -ca2_im], axis=0).T.astype(wc_ref.dtype)
        k01 = (lax.dot_general(bbar_re, jnp.concatenate([c_re, ca_re], axis=0), nt,
                               precision=lax.Precision.HIGHEST, preferred_element_type=jnp.float32)
               - lax.dot_general(bbar_im, jnp.concatenate([c_im, ca_im], axis=0), nt,
                                 precision=lax.Precision.HIGHEST, preferred_element_type=jnp.float32))
        wc_ref[j, SLAB_LANES:SLAB_LANES + LANES, :] = k01.astype(wc_ref.dtype)
        wc_ref[j, SLAB_LANES + LANES:WC_ROWS, :] = jnp.concatenate(
            [jnp.zeros((LANES, LANES), jnp.float32), k01[:, 0:LANES]], axis=1).astype(wc_ref.dtype)


def _prepare(log_dt, lam_re, lam_im, b_re, b_im, c_re, c_im):
    f32 = jnp.float32
    depth = log_dt.shape[0]
    ldt = jnp.broadcast_to(log_dt.reshape(depth, N_SLABS, GROUPS_PER_SLAB, 1),
                           (depth, N_SLABS, GROUPS_PER_SLAB, SSM_STATE))

    def rows(v):
        return v.reshape(depth, N_SLABS, 1, SLAB_STATE)

    def tiled(t):
        t = t.transpose(0, 1, 3, 2, 4).reshape(depth, N_SLABS, 1, SSM_GROUP_SIZE, SLAB_STATE)
        t = jnp.broadcast_to(t, (depth, N_SLABS, GROUPS_PER_SLAB, SSM_GROUP_SIZE, SLAB_STATE))
        return t.reshape(depth, N_SLABS, LANES, SLAB_STATE)

    def b_tiled(b):
        return tiled(b.reshape(depth, N_SLABS, GROUPS_PER_SLAB, SSM_STATE, SSM_GROUP_SIZE).transpose(0, 1, 2, 4, 3))

    def c_tiled(c):
        return tiled(c.reshape(depth, N_SLABS, GROUPS_PER_SLAB, SSM_GROUP_SIZE, SSM_STATE))

    def spec(*shape):
        zeros = (0,) * len(shape)
        return pl.BlockSpec((None,) + shape, lambda l: (l,) + zeros)

    vec_spec, mat_spec = spec(N_SLABS, 1, SLAB_STATE), spec(N_SLABS, LANES, SLAB_STATE)
    return pl.pallas_call(
        _prepare_kernel,
        grid=(depth,),
        in_specs=[vec_spec] * 3 + [mat_spec] * 4,
        out_specs=(spec(SUBLANES, STATE_LANES), spec(N_SLABS, WB_ROWS, SLAB_LANES),
                   spec(N_SLABS, WC_ROWS, PAIR * LANES)),
        out_shape=(jax.ShapeDtypeStruct((depth, SUBLANES, STATE_LANES), f32),
                   jax.ShapeDtypeStruct((depth, N_SLABS, WB_ROWS, SLAB_LANES), jnp.bfloat16),
                   jax.ShapeDtypeStruct((depth, N_SLABS, WC_ROWS, PAIR * LANES), jnp.bfloat16)),
        name="s5_prepare",
    )(rows(ldt), rows(lam_re), rows(lam_im), b_tiled(b_re), b_tiled(b_im), c_tiled(c_re), c_tiled(c_im))


def _rmsnorm(x, g):
    return x * lax.rsqrt(jnp.mean(x * x, axis=-1, keepdims=True) + EPS) * g


def _split_pairs(v):
    pairs = v.shape[0] // (PAIR * BATCH)
    v = v.reshape(pairs, PAIR, BATCH, v.shape[-1])
    return v[:, 0].reshape(pairs * BATCH, -1), v[:, 1].reshape(pairs * BATCH, -1)


def _layer_kernel(x_ref, g_ref, w_in_ref, b_in_ref, a2_ref, wb_ref, wc_ref, d_ref, w_glu_ref,
                  b_glu_ref, pool_w_ref, pool_scale_ref, w_a_ref, w_b_ref, w_out_ref, fin_g_ref,
                  o_ref, s_ref, y_ref, state_ref, ub_ref, *, first, last):
    step = pl.program_id(0)
    bf16 = jnp.bfloat16
    col_za, col_ub, col_zb, col_ga, col_gb = (SSM_WIDTH, 2 * SSM_WIDTH, 2 * SSM_WIDTH + POOL_WIDTH,
                                              2 * SSM_WIDTH + 2 * POOL_WIDTH,
                                              2 * SSM_WIDTH + 2 * POOL_WIDTH + D_MODEL)

    @pl.when(step == 0)
    def _():
        state_ref[...] = jnp.zeros_like(state_ref)
        ub_ref[0:HALO_ROWS, :] = jnp.zeros((HALO_ROWS, POOL_WIDTH), jnp.float32)

    v = [dict() for _ in range(N_SUB)]

    def load(i):
        if first:
            x = jnp.swapaxes(x_ref[:, i * SUB_STEPS:(i + 1) * SUB_STEPS, :], 0, 1).reshape(SUB_ROWS, D_MODEL)
        else:
            x = x_ref[i * SUB_ROWS:(i + 1) * SUB_ROWS, :]
        v[i]["x"] = x
        v[i]["hb"] = _rmsnorm(x, g_ref[...]).astype(bf16)

    def proj(i, lo, hi):
        return _dot(v[i]["hb"], w_in_ref[:, lo:hi]) + b_in_ref[:, lo:hi]

    def ssm_in(i):
        ua = proj(i, 0, SSM_WIDTH)
        u_even, u_odd = _split_pairs(ua)
        v[i]["ua"] = ua
        v[i]["uu"] = [jnp.concatenate([u_even[:, LANES * j:LANES * (j + 1)], u_odd[:, LANES * j:LANES * (j + 1)]],
                                      axis=1).astype(bf16) for j in range(N_SLABS)]

    def pool_in(i):
        ub = proj(i, col_ub, col_zb)
        base = HALO_ROWS + i * SUB_ROWS
        ub_ref[base:base + SUB_ROWS, :] = ub
        t_abs = (step * TIME_TILE + i * SUB_STEPS
                 + lax.broadcasted_iota(jnp.int32, (SUB_ROWS, 1), 0) // BATCH)
        pooled = []
        for gi, win in enumerate(POOL_WINDOWS):
            lanes = slice(gi * POOL_GROUP, (gi + 1) * POOL_GROUP)
            tok = ub[:, lanes]
            acc = tok
            for back in range(1, win):
                lo = base - back * BATCH
                acc = acc + ub_ref[lo:lo + SUB_ROWS, lanes]
            count = jnp.minimum(t_abs + 1, win).astype(jnp.float32)
            pooled.append((acc / count - tok).astype(bf16))
        v[i]["pooled"] = pooled

    def drive():
        uu = [jnp.concatenate([v[i]["uu"][j] for i in range(N_SUB)], axis=0) for j in range(N_SLABS)]
        for j in range(N_SLABS):
            s_ref[:, SLAB_LANES * j:SLAB_LANES * (j + 1)] = _dot(uu[j], wb_ref[j])
        for i in range(N_SUB):
            mixed = [_dot(jnp.concatenate(v[i]["pooled"][2 * k:2 * k + 2], axis=1), pool_w_ref[k])
                     for k in range(POOL_PAIRS)]
            v[i]["mixed"] = jnp.concatenate(mixed, axis=1) * pool_scale_ref[...]
        return uu

    def recurrence():
        for j in range(N_SLABS):
            re_lanes = slice(SLAB_LANES * j, SLAB_LANES * j + SLAB_STATE)
            im_lanes = slice(SLAB_LANES * j + SLAB_STATE, SLAB_LANES * (j + 1))
            a_re = a2_ref[:, re_lanes]
            a_im = a2_ref[:, im_lanes]
            s_re = state_ref[:, re_lanes]
            s_im = state_ref[:, im_lanes]
            for q in range(PAIRS):
                rows = slice(q * BATCH, (q + 1) * BATCH)
                n_re = a_re * s_re - a_im * s_im + s_ref[rows, re_lanes]
                n_im = a_re * s_im + a_im * s_re + s_ref[rows, im_lanes]
                s_ref[rows, re_lanes] = s_re
                s_ref[rows, im_lanes] = s_im
                s_re, s_im = n_re, n_im
            state_ref[:, re_lanes] = s_re
            state_ref[:, im_lanes] = s_im

    def gates_z(i):
        v[i]["silu_za"] = jax.nn.silu(proj(i, col_za, col_ub))
        v[i]["yb"] = (v[i]["mixed"] * jax.nn.silu(proj(i, col_zb, col_ga))).astype(bf16)

    def ssm_out(uu):
        for j in range(N_SLABS):
            yy = (_dot(s_ref[:, SLAB_LANES * j:SLAB_LANES * (j + 1)].astype(bf16), wc_ref[j, 0:SLAB_LANES, :])
                  + _dot(uu[j], wc_ref[j, SLAB_LANES:WC_ROWS, :]))
            lanes = slice(LANES * j, LANES * (j + 1))
            y_ref[:, 0:BATCH, lanes] = yy[:, 0:LANES].reshape(PAIRS, BATCH, LANES)
            y_ref[:, BATCH:PAIR * BATCH, lanes] = yy[:, LANES:PAIR * LANES].reshape(PAIRS, BATCH, LANES)
        for i in range(N_SUB):
            y = y_ref[i * SUB_PAIRS:(i + 1) * SUB_PAIRS].reshape(SUB_ROWS, SSM_WIDTH)
            v[i]["y"] = jax.nn.gelu(y + d_ref[...] * v[i]["ua"])

    def gate_a(i):
        v[i]["sig_ga"] = jax.nn.sigmoid(proj(i, col_ga, col_gb))

    def glu(i):
        y = v[i]["y"]
        ya = y * jax.nn.sigmoid(_dot(y.astype(bf16), w_glu_ref[...]) + b_glu_ref[...])
        v[i]["ya"] = (ya * v[i]["silu_za"]).astype(bf16)

    def gate_b(i):
        v[i]["sig_gb"] = jax.nn.sigmoid(proj(i, col_gb, N_IN))

    def merge(i):
        merged = (v[i]["sig_ga"] * _dot(v[i]["ya"], w_a_ref[...])
                  + v[i]["sig_gb"] * _dot(v[i]["yb"], w_b_ref[...]))
        v[i]["merged"] = merged.astype(bf16)

    def finish(i):
        out = v[i]["x"] + _dot(v[i]["merged"], w_out_ref[...])
        if last:
            out = _rmsnorm(out, fin_g_ref[...])
            o_ref[:, i * SUB_STEPS:(i + 1) * SUB_STEPS, :] = jnp.swapaxes(
                out.reshape(SUB_STEPS, BATCH, D_MODEL), 0, 1)
        else:
            o_ref[i * SUB_ROWS:(i + 1) * SUB_ROWS, :] = out

    subs = range(N_SUB)
    for i in subs:
        load(i)
    for i in subs:
        ssm_in(i)
        pool_in(i)
    ub_ref[0:HALO_ROWS, :] = ub_ref[ROWS:ROWS + HALO_ROWS, :]
    uu = drive()
    recurrence()
    for i in subs:
        gates_z(i)
    ssm_out(uu)
    for i in subs:
        gate_a(i)
    for i in subs:
        glu(i)
        gate_b(i)
    for i in subs:
        merge(i)
    for i in subs:
        finish(i)


def _layer_spec(arr, layer):
    zeros = (0,) * (arr.ndim - 1)
    return pl.BlockSpec((None,) + arr.shape[1:], lambda i: (layer,) + zeros, pipeline_mode=pl.Buffered(1))


def _layer(xt, layer, per_layer, fin_g, *, first, last):
    time_major = pl.BlockSpec((ROWS, D_MODEL), lambda i: (i, 0))
    batch_major = pl.BlockSpec((BATCH, TIME_TILE, D_MODEL), lambda i: (0, i, 0))
    fin_spec = pl.BlockSpec(fin_g.shape, lambda i: (0, 0), pipeline_mode=pl.Buffered(1))
    return pl.pallas_call(
        functools.partial(_layer_kernel, first=first, last=last),
        grid=(SEQ // TIME_TILE,),
        in_specs=([batch_major if first else time_major] + [_layer_spec(p, layer) for p in per_layer]
                  + [fin_spec]),
        out_specs=batch_major if last else time_major,
        out_shape=jax.ShapeDtypeStruct((BATCH, SEQ, D_MODEL) if last else (SEQ * BATCH, D_MODEL),
                                       jnp.float32),
        scratch_shapes=[pltpu.VMEM((PAIRS * BATCH, STATE_LANES), jnp.float32),
                        pltpu.VMEM((PAIRS, PAIR * BATCH, SSM_WIDTH), jnp.float32),
                        pltpu.VMEM((SUBLANES, STATE_LANES), jnp.float32),
                        pltpu.VMEM((HALO_ROWS + ROWS, POOL_WIDTH), jnp.float32)],
        compiler_params=pltpu.CompilerParams(dimension_semantics=("arbitrary",),
                                             vmem_limit_bytes=VMEM_LIMIT_BYTES),
        name=f"hybrid_layer_{int(first)}{int(last)}",
    )(xt, *per_layer, fin_g)


def _pool_pairs(pool_w):
    depth = pool_w.shape[0]
    w = pool_w.reshape(depth, POOL_PAIRS, 2, 1, POOL_GROUP, POOL_GROUP)
    w = jnp.broadcast_to(w, (depth, POOL_PAIRS, 2, 2, POOL_GROUP, POOL_GROUP))
    keep = (jnp.arange(2)[:, None] == jnp.arange(2)[None, :])[None, None, :, :, None, None]
    w = jnp.where(keep, w, 0.0).transpose(0, 1, 2, 4, 3, 5)
    return w.reshape(depth, POOL_PAIRS, 2 * POOL_GROUP, 2 * POOL_GROUP)


def kernel(x, norm_g, w_in, b_in, ssm_log_dt, ssm_lam_re, ssm_lam_im, ssm_b_re, ssm_b_im, ssm_c_re, ssm_c_im, ssm_d, ssm_w_glu, ssm_b_glu, pool_w, pool_scale, w_branch_a, w_branch_b, w_out, final_norm_g):
    bf16 = jnp.bfloat16
    a2, wb, wc = _prepare(ssm_log_dt, ssm_lam_re, ssm_lam_im, ssm_b_re, ssm_b_im, ssm_c_re, ssm_c_im)

    def vec(v):
        return v.reshape(DEPTH, 1, v.shape[-1])

    per_layer = (vec(norm_g), w_in.astype(bf16), vec(b_in), a2, wb, wc, vec(ssm_d), ssm_w_glu.astype(bf16),
                 vec(ssm_b_glu), _pool_pairs(pool_w).astype(bf16), vec(pool_scale), w_branch_a.astype(bf16),
                 w_branch_b.astype(bf16), w_out.astype(bf16))
    fin_g = final_norm_g.reshape(1, D_MODEL)
    xt = x
    for l in range(DEPTH):
        xt = _layer(xt, l, per_layer, fin_g, first=(l == 0), last=(l == DEPTH - 1))
    return xt
```
